```python
import math
import jax, jax.numpy as jnp
from jax import lax
import numpy as np

D_MODEL = 1024
BATCH = 4
SEQ = 4096
DEPTH = 4
DEC_BATCH = 32
DEC_SEQ = 1
PAST_LEN = 8192
PAGE_SIZE = 128

F32 = jnp.float32
EPS = 1e-6
N_EVEN = (DEPTH + 1) // 2
N_ODD = DEPTH // 2

GDN_HEADS = 4
GDN_DK = 128
GDN_DV = 128
GDN_CONV = 4
GDN_CHUNK = 64
GDN_QKV = GDN_HEADS * (2 * GDN_DK + GDN_DV)

SWA_GROUPS = ((128, 1), (512, 4), (2048, 16))
N_SWA_GROUPS = 3
SWA_HEADS = 4
SWA_HD = 128
SWA_GW = SWA_HEADS * SWA_HD

LRU_WIDTH = D_MODEL
LRU_BLOCKS = 4
LRU_BW = LRU_WIDTH // LRU_BLOCKS
LRU_CONV = 4
LRU_C = 8.0

MEM_LEN = 256
MEM_HEADS = 4
MEM_HD = 128
MEM_W = MEM_HEADS * MEM_HD

EVEN_SIZES = (GDN_QKV, GDN_HEADS * GDN_DV, GDN_HEADS, GDN_HEADS,
              3 * N_SWA_GROUPS * SWA_GW, SWA_GW, MEM_W, MEM_W)
EVEN_IN = sum(EVEN_SIZES)
EVEN_MIX = GDN_HEADS * GDN_DV + SWA_GW + MEM_W
ODD_SIZES = (LRU_WIDTH, LRU_WIDTH, MEM_W, MEM_W)
ODD_IN = sum(ODD_SIZES)
ODD_MIX = LRU_WIDTH + MEM_W

kernel_name = 'hybrid_gdn_dilated_rglru_mem_decoder_step'


def rms_norm(x, g):
    xf = x.astype(F32)
    y = xf * lax.rsqrt(jnp.mean(xf * xf, axis=-1, keepdims=True) + EPS)
    return (y * g.astype(F32)).astype(x.dtype)


def l2norm(x):
    return x * lax.rsqrt(jnp.sum(x * x, axis=-1, keepdims=True) + EPS)


def split_cols(h, sizes):
    offs = np.cumsum(np.array(sizes))[:-1].tolist()
    return jnp.split(h, offs, axis=-1)


def causal_dwconv(x, buf, w):
    K = w.shape[0]
    T = x.shape[1]
    xp = jnp.concatenate([buf.astype(x.dtype), x], axis=1)
    y = xp[:, 0:T] * w[0]
    for j in range(1, K):
        y = y + xp[:, j:j + T] * w[j]
    return y, xp[:, T:]


def gdn_recurrent(q, k, v, beta, g, S0):
    def step(S, inp):
        qt, kt, vt, bt, gt = inp
        S = S * jnp.exp(gt)[..., None, None]
        u = (vt - jnp.einsum('bhk,bhkv->bhv', kt, S)) * bt[..., None]
        S = S + jnp.einsum('bhk,bhv->bhkv', kt, u)
        return S, jnp.einsum('bhk,bhkv->bhv', qt, S)
    xs = (jnp.moveaxis(q, 1, 0), jnp.moveaxis(k, 1, 0), jnp.moveaxis(v, 1, 0),
          jnp.moveaxis(beta, 1, 0), jnp.moveaxis(g, 1, 0))
    S, o = lax.scan(step, S0, xs)
    return jnp.moveaxis(o, 0, 1), S


def gdn_chunked(q, k, v, beta, g, S0):
    B, T, H, dk = q.shape
    dv = v.shape[-1]
    C = GDN_CHUNK
    N = T // C

    def chunks(t):
        t = t.reshape((B, N, C, H) + t.shape[3:])
        return jnp.moveaxis(t, (1, 3), (0, 2))

    qc, kc, vc, bc = chunks(q), chunks(k), chunks(v), chunks(beta)
    gc = jnp.cumsum(chunks(g), axis=-1)
    pos = jnp.arange(C)
    incl = pos[:, None] >= pos[None, :]
    strict = pos[:, None] > pos[None, :]
    diff = gc[..., :, None] - gc[..., None, :]
    decay = jnp.where(incl, jnp.exp(jnp.where(incl, diff, 0.0)), 0.0)
    kb = kc * bc[..., None]
    a_low = jnp.einsum('nbhik,nbhjk->nbhij', kb, kc) * jnp.where(strict, decay, 0.0)
    rhs = jnp.concatenate([vc * bc[..., None], kb * jnp.exp(gc)[..., None]], axis=-1)
    sol = lax.linalg.triangular_solve(a_low + jnp.eye(C, dtype=F32), rhs,
                                      left_side=True, lower=True, unit_diagonal=True)
    u, w = sol[..., :dv], sol[..., dv:]
    attn = jnp.einsum('nbhik,nbhjk->nbhij', qc, kc) * decay

    def step(S, inp):
        qi, ki, ui, wi, ai, gi = inp
        v_new = ui - jnp.einsum('bhck,bhkv->bhcv', wi, S)
        o = (jnp.einsum('bhck,bhkv->bhcv', qi * jnp.exp(gi)[..., None], S)
             + jnp.einsum('bhij,bhjv->bhiv', ai, v_new))
        g_last = gi[..., -1]
        k_dec = ki * jnp.exp(g_last[..., None] - gi)[..., None]
        S = S * jnp.exp(g_last)[..., None, None] + jnp.einsum('bhck,bhcv->bhkv', k_dec, v_new)
        return S, o

    S, o = lax.scan(step, S0, (qc, kc, u, w, attn, gc))
    o = jnp.moveaxis(o, (0, 2), (1, 3)).reshape(B, T, H, dv)
    return o, S


def dilated_group_prompt(q, k, v, window, dil):
    B, S, H, hd = q.shape
    n = window // dil
    span = n * dil
    Sp = -(-S // span) * span
    NB = Sp // span

    def blocks(t):
        t = jnp.pad(t, ((0, 0), (0, Sp - S), (0, 0), (0, 0))).reshape(B, NB, n, dil, H, hd)
        return jnp.moveaxis(t, 3, 1)

    qb, kb, vb = blocks(q), blocks(k), blocks(v)

    def with_prev(t):
        prev = jnp.pad(t, ((0, 0), (0, 0), (1, 0), (0, 0), (0, 0), (0, 0)))[:, :, :-1]
        return jnp.concatenate([prev, t], axis=3)

    kc, vc = with_prev(kb), with_prev(vb)
    s = jnp.einsum('brnqhd,brnkhd->brnhqk', qb, kc, preferred_element_type=F32) * hd ** -0.5
    qi = jnp.arange(n)[:, None]
    kj = jnp.arange(2 * n)[None, :]
    dist = n + qi - kj
    band = (dist >= 0) & (dist <= n)
    mask = band[None] & ((jnp.arange(NB)[:, None, None] > 0) | (kj >= n)[None])
    s = jnp.where(mask[None, None, :, None], s, -jnp.inf)
    m = jnp.max(s, axis=-1)
    p = jnp.exp(s - m[..., None])
    l = jnp.sum(p, axis=-1)
    num = jnp.einsum('brnhqk,brnkhd->brnqhd', p, vc.astype(F32))

    def unblock(t):
        t = jnp.moveaxis(t, 1, 3)
        return t.reshape((B, Sp) + t.shape[4:])[:, :S]

    return unblock(num), unblock(jnp.swapaxes(m, -1, -2)), unblock(jnp.swapaxes(l, -1, -2))


def dilated_group_sample(q, k, v, buf, window, dil):
    B, T, H, hd = q.shape
    L = buf.shape[1]
    n = window // dil
    cat = jnp.concatenate([buf.astype(k.dtype), jnp.stack([k, v], axis=2)], axis=1)
    idx = L + jnp.arange(T)[:, None] - dil * jnp.arange(n + 1)[None, :]
    valid = idx >= 0
    kv = cat[:, jnp.maximum(idx, 0)]
    s = jnp.einsum('bthd,btjhd->bthj', q, kv[:, :, :, 0], preferred_element_type=F32) * hd ** -0.5
    s = jnp.where(valid[None, :, None, :], s, -jnp.inf)
    m = jnp.max(s, axis=-1)
    p = jnp.exp(s - m[..., None])
    num = jnp.einsum('bthj,btjhd->bthd', p, kv[:, :, :, 1].astype(F32))
    return (num, m, jnp.sum(p, axis=-1)), cat[:, T:]


def combine_dilations(parts):
    m_all = parts[0][1]
    for part in parts[1:]:
        m_all = jnp.maximum(m_all, part[1])
    num = jnp.zeros_like(parts[0][0])
    den = jnp.zeros_like(parts[0][2])
    for nu, m, l in parts:
        c = jnp.exp(m - m_all)
        num = num + nu * c[..., None]
        den = den + l * c
    return num / den[..., None]


def memory_kv(mem, g, w_kv):
    B, M, _ = mem.shape
    return (rms_norm(mem, g) @ w_kv).reshape(B, M, 2, MEM_HEADS, MEM_HD)


def memory_attend(q, kv):
    s = jnp.einsum('bthd,bmhd->bhtm', q, kv[:, :, 0], preferred_element_type=F32) * MEM_HD ** -0.5
    p = jax.nn.softmax(s, axis=-1)
    return jnp.einsum('bhtm,bmhd->bthd', p, kv[:, :, 1].astype(F32))


def linear_scan(a, b, h0):
    def op(l, r):
        return (l[0] * r[0], r[0] * l[1] + r[1])
    A, Bc = lax.associative_scan(op, (a, b), axis=1)
    return A * h0[:, None] + Bc


def rg_lru(xc, wa, ba, wx, bx, lam, h0):
    B, T, _ = xc.shape
    xb = xc.astype(F32).reshape(B, T, LRU_BLOCKS, LRU_BW)
    r = jax.nn.sigmoid(jnp.einsum('btnc,ncd->btnd', xb, wa.astype(F32)) + ba.astype(F32))
    i = jax.nn.sigmoid(jnp.einsum('btnc,ncd->btnd', xb, wx.astype(F32)) + bx.astype(F32))
    log_a = -LRU_C * r * jax.nn.softplus(-lam.astype(F32).reshape(LRU_BLOCKS, LRU_BW))
    a = jnp.exp(log_a)
    b = jnp.sqrt(-jnp.expm1(2.0 * log_a)) * (i * xb)
    h = linear_scan(a.reshape(B, T, LRU_WIDTH), b.reshape(B, T, LRU_WIDTH), h0.astype(F32))
    return h, h[:, -1]


def even_mixer(h, mkv, S0, conv0, swa_bufs, w_in, conv_w, a_log, dt_bias, o_norm, prompt):
    B, T, _ = h.shape
    qkv_a, z_a, b_a, a_a, qkv_b, gate_b, q_m, gate_m = split_cols(h @ w_in, EVEN_SIZES)
    c, conv_new = causal_dwconv(qkv_a, conv0, conv_w)
    c = jax.nn.silu(c.astype(F32))
    qa, ka, va = jnp.split(c, [GDN_HEADS * GDN_DK, 2 * GDN_HEADS * GDN_DK], axis=-1)
    qa = l2norm(qa.reshape(B, T, GDN_HEADS, GDN_DK)) * GDN_DK ** -0.5
    ka = l2norm(ka.reshape(B, T, GDN_HEADS, GDN_DK))
    va = va.reshape(B, T, GDN_HEADS, GDN_DV)
    beta = jax.nn.sigmoid(b_a.astype(F32))
    g = -jnp.exp(a_log.astype(F32)) * jax.nn.softplus(a_a.astype(F32) + dt_bias.astype(F32))
    if prompt:
        o_a, S_new = gdn_chunked(qa, ka, va, beta, g, S0.astype(F32))
    else:
        o_a, S_new = gdn_recurrent(qa, ka, va, beta, g, S0.astype(F32))
    o_a = rms_norm(o_a, o_norm) * jax.nn.silu(z_a.astype(F32)).reshape(B, T, GDN_HEADS, GDN_DV)
    qkv_b = qkv_b.reshape(B, T, 3, N_SWA_GROUPS, SWA_HEADS, SWA_HD)
    parts, bufs_new = [], []
    for gi, (win, dil) in enumerate(SWA_GROUPS):
        qg, kg, vg = qkv_b[:, :, 0, gi], qkv_b[:, :, 1, gi], qkv_b[:, :, 2, gi]
        if prompt:
            parts.append(dilated_group_prompt(qg, kg, vg, win, dil))
            keep = min(win, T)
            bufs_new.append(jnp.stack([kg, vg], axis=2)[:, T - keep:])
        else:
            part, buf = dilated_group_sample(qg, kg, vg, swa_bufs[gi], win, dil)
            parts.append(part)
            bufs_new.append(buf)
    o_b = combine_dilations(parts) * jax.nn.silu(gate_b.astype(F32)).reshape(B, T, SWA_HEADS, SWA_HD)
    o_m = memory_attend(q_m.reshape(B, T, MEM_HEADS, MEM_HD), mkv)
    o_m = o_m * jax.nn.silu(gate_m.astype(F32)).reshape(B, T, MEM_HEADS, MEM_HD)
    mix = jnp.concatenate([o_a.reshape(B, T, -1), o_b.reshape(B, T, -1), o_m.reshape(B, T, -1)], axis=-1)
    return mix, S_new, conv_new, bufs_new


def odd_mixer(h, mkv, h0, conv0, w_in, conv_w, conv_b, wa, ba, wx, bx, lam):
    B, T, _ = h.shape
    xb, gate, q_m, gate_m = split_cols(h @ w_in, ODD_SIZES)
    xc, conv_new = causal_dwconv(xb, conv0, conv_w)
    hs, h_last = rg_lru(xc + conv_b, wa, ba, wx, bx, lam, h0)
    o_c = hs * jax.nn.silu(gate.astype(F32))
    o_m = memory_attend(q_m.reshape(B, T, MEM_HEADS, MEM_HD), mkv)
    o_m = o_m * jax.nn.silu(gate_m.astype(F32)).reshape(B, T, MEM_HEADS, MEM_HD)
    mix = jnp.concatenate([o_c, o_m.reshape(B, T, -1)], axis=-1)
    return mix, h_last, conv_new


def setup_inputs(seed: int = 0) -> dict:
    key = jax.random.key(seed)
    ks = iter(jax.random.split(key, 40))

    def nrm(shape, scale=1.0):
        return jax.random.normal(next(ks), shape, F32) * scale

    def uni(shape, lo, hi):
        return jax.random.uniform(next(ks), shape, F32, lo, hi)

    swa_len = [min(w, PAST_LEN) for w, _ in SWA_GROUPS]
    dt = jnp.exp(uni((N_EVEN, GDN_HEADS), math.log(1e-3), math.log(1e-1)))
    s_lru = uni((N_ODD, LRU_WIDTH), 0.81, 0.998) ** (1.0 / LRU_C)
    return {
        'x_prompt': nrm((BATCH, SEQ, D_MODEL)),
        'x_sample': nrm((DEC_BATCH, DEC_SEQ, D_MODEL)),
        'state_gdn': nrm((N_EVEN, DEC_BATCH, GDN_HEADS, GDN_DK, GDN_DV), 0.1),
        'state_gdn_conv': nrm((N_EVEN, DEC_BATCH, GDN_CONV - 1, GDN_QKV)),
        'cache_swa1': nrm((N_EVEN, DEC_BATCH, swa_len[0], 2, SWA_HEADS, SWA_HD)),
        'cache_swa2': nrm((N_EVEN, DEC_BATCH, swa_len[1], 2, SWA_HEADS, SWA_HD)),
        'cache_swa3': nrm((N_EVEN, DEC_BATCH, swa_len[2], 2, SWA_HEADS, SWA_HD)),
        'state_lru': nrm((N_ODD, DEC_BATCH, LRU_WIDTH), 0.5),
        'state_lru_conv': nrm((N_ODD, DEC_BATCH, LRU_CONV - 1, LRU_WIDTH)),
        'cache_mem': nrm((DEPTH, DEC_BATCH, MEM_LEN, 2, MEM_HEADS, MEM_HD)),
        'mem_prompt': nrm((BATCH, MEM_LEN, D_MODEL)),
        'norm_pre': 1.0 + nrm((DEPTH, D_MODEL), 0.02),
        'norm_post': 1.0 + nrm((DEPTH, D_MODEL), 0.02),
        'mem_norm': 1.0 + nrm((DEPTH, D_MODEL), 0.02),
        'w_mem_kv': nrm((DEPTH, D_MODEL, 2 * MEM_W), D_MODEL ** -0.5),
        'w_in_even': nrm((N_EVEN, D_MODEL, EVEN_IN), D_MODEL ** -0.5),
        'w_out_even': nrm((N_EVEN, EVEN_MIX, D_MODEL), EVEN_MIX ** -0.5),
        'gdn_conv_w': nrm((N_EVEN, GDN_CONV, GDN_QKV), 0.5),
        'gdn_a_log': jnp.log(uni((N_EVEN, GDN_HEADS), 1.0, 16.0)),
        'gdn_dt_bias': dt + jnp.log(-jnp.expm1(-dt)),
        'gdn_norm': 1.0 + nrm((N_EVEN, GDN_DV), 0.02),
        'w_in_odd': nrm((N_ODD, D_MODEL, ODD_IN), D_MODEL ** -0.5),
        'w_out_odd': nrm((N_ODD, ODD_MIX, D_MODEL), ODD_MIX ** -0.5),
        'lru_conv_w': nrm((N_ODD, LRU_CONV, LRU_WIDTH), 0.5),
        'lru_conv_b': nrm((N_ODD, LRU_WIDTH), 0.01),
        'lru_wa': nrm((N_ODD, LRU_BLOCKS, LRU_BW, LRU_BW), LRU_BW ** -0.5),
        'lru_ba': nrm((N_ODD, LRU_BLOCKS, LRU_BW), 0.01),
        'lru_wx': nrm((N_ODD, LRU_BLOCKS, LRU_BW, LRU_BW), LRU_BW ** -0.5),
        'lru_bx': nrm((N_ODD, LRU_BLOCKS, LRU_BW), 0.01),
        'lru_lambda': jnp.log(s_lru) - jnp.log1p(-s_lru),
    }


def reference(x_prompt, x_sample, state_gdn, state_gdn_conv, cache_swa1, cache_swa2, cache_swa3,
              state_lru, state_lru_conv, cache_mem, mem_prompt,
              norm_pre, norm_post, mem_norm, w_mem_kv,
              w_in_even, w_out_even, gdn_conv_w, gdn_a_log, gdn_dt_bias, gdn_norm,
              w_in_odd, w_out_odd, lru_conv_w, lru_conv_b, lru_wa, lru_ba, lru_wx, lru_bx, lru_lambda):
    Bp = x_prompt.shape[0]
    xp, xs = x_prompt, x_sample
    swa_in = (cache_swa1, cache_swa2, cache_swa3)
    gdn_p, gdn_s, gconv_p, gconv_s = [], [], [], []
    swa_p = ([], [], [])
    swa_s = ([], [], [])
    lru_p, lru_s, lconv_p, lconv_s, mem_p = [], [], [], [], []
    for layer in range(DEPTH):
        j = layer // 2
        mkv_p = memory_kv(mem_prompt, mem_norm[layer], w_mem_kv[layer])
        mem_p.append(mkv_p)
        hp = rms_norm(xp, norm_pre[layer])
        hs = rms_norm(xs, norm_pre[layer])
        if layer % 2 == 0:
            ew = (w_in_even[j], gdn_conv_w[j], gdn_a_log[j], gdn_dt_bias[j], gdn_norm[j])
            mix_p, S_p, c_p, b_p = even_mixer(
                hp, mkv_p, jnp.zeros((Bp, GDN_HEADS, GDN_DK, GDN_DV), F32),
                jnp.zeros((Bp, GDN_CONV - 1, GDN_QKV), xp.dtype), None, *ew, True)
            mix_s, S_s, c_s, b_s = even_mixer(
                hs, cache_mem[layer], state_gdn[j], state_gdn_conv[j],
                [c[j] for c in swa_in], *ew, False)
            gdn_p.append(S_p)
            gdn_s.append(S_s)
            gconv_p.append(c_p)
            gconv_s.append(c_s)
            for gi in range(N_SWA_GROUPS):
                swa_p[gi].append(b_p[gi])
                swa_s[gi].append(b_s[gi])
            w_out = w_out_even[j]
        else:
            ow = (w_in_odd[j], lru_conv_w[j], lru_conv_b[j], lru_wa[j], lru_ba[j],
                  lru_wx[j], lru_bx[j], lru_lambda[j])
            mix_p, h_p, c_p = odd_mixer(
                hp, mkv_p, jnp.zeros((Bp, LRU_WIDTH), F32),
                jnp.zeros((Bp, LRU_CONV - 1, LRU_WIDTH), xp.dtype), *ow)
            mix_s, h_s, c_s = odd_mixer(hs, cache_mem[layer], state_lru[j], state_lru_conv[j], *ow)
            lru_p.append(h_p)
            lru_s.append(h_s)
            lconv_p.append(c_p)
            lconv_s.append(c_s)
            w_out = w_out_odd[j]
        xp = xp + rms_norm(mix_p.astype(xp.dtype) @ w_out, norm_post[layer])
        xs = xs + rms_norm(mix_s.astype(xs.dtype) @ w_out, norm_post[layer])
    return (xp, xs,
            jnp.stack(gdn_p), jnp.stack(gdn_s), jnp.stack(gconv_p), jnp.stack(gconv_s),
            jnp.stack(swa_p[0]), jnp.stack(swa_s[0]), jnp.stack(swa_p[1]), jnp.stack(swa_s[1]),
            jnp.stack(swa_p[2]), jnp.stack(swa_s[2]),
            jnp.stack(lru_p), jnp.stack(lru_s), jnp.stack(lconv_p), jnp.stack(lconv_s),
            jnp.stack(mem_p))
```

```python
import functools
import math

import jax
import jax.numpy as jnp
from jax import lax
from jax.experimental import pallas as pl
from jax.experimental.pallas import tpu as pltpu

F32 = jnp.float32
BF16 = jnp.bfloat16
EPS = 1e-6

D_MODEL = 1024
HEADS = 4
HD = 128
GW = HEADS * HD
GDN_QKV = 3 * GW
GDN_CHUNK = 64
SWA_GROUPS = ((128, 1), (512, 4), (2048, 16))
SWA_N = 128
MEM_LEN = 256
LRU_WIDTH = 1024
LRU_BLOCKS = 4
LRU_BW = 256
LRU_C = 8.0
EVEN_MAIN = 8192
ODD_IN = 3072
SCALE = HD ** -0.5
VMEM_LIMIT = 48 * 1024 * 1024

HIGHEST = lax.Precision.HIGHEST


def _params(*sem):
    return pltpu.CompilerParams(dimension_semantics=sem, vmem_limit_bytes=VMEM_LIMIT)


def _silu(x):
    return x * jax.nn.sigmoid(x)


def _softplus(x):
    return jnp.maximum(x, 0.0) + jnp.log1p(jnp.exp(-jnp.abs(x)))


def _dot(a, b):
    return jnp.dot(a.astype(BF16), b.astype(BF16), preferred_element_type=F32)


def _dot_nt(a, b):
    return lax.dot_general(a.astype(BF16), b.astype(BF16), (((1,), (1,)), ((), ())),
                           preferred_element_type=F32)


def _dot_f32(a, b):
    return jnp.dot(a, b, precision=HIGHEST, preferred_element_type=F32)


def _row_to_col(row, n):
    eye = lax.broadcasted_iota(jnp.int32, (n, n), 0) == lax.broadcasted_iota(jnp.int32, (n, n), 1)
    return jnp.sum(jnp.where(eye, jnp.broadcast_to(row, (n, n)), 0.0), axis=1, keepdims=True)


def _norm_matmul_kernel(x_ref, g_ref, w_ref, o_ref, h_ref):
    @pl.when(pl.program_id(1) == 0)
    def _():
        x = x_ref[...]
        y = x * lax.rsqrt(jnp.mean(x * x, axis=-1, keepdims=True) + EPS) * g_ref[...]
        h_ref[...] = y.astype(BF16)

    o_ref[...] = jnp.dot(h_ref[...], w_ref[...], preferred_element_type=F32)


def norm_matmul(x, g, w, tm, tn):
    n, d = x.shape
    c = w.shape[1]
    return pl.pallas_call(
        _norm_matmul_kernel,
        grid=(n // tm, c // tn),
        in_specs=[pl.BlockSpec((tm, d), lambda i, j: (i, 0)),
                  pl.BlockSpec((1, d), lambda i, j: (0, 0)),
                  pl.BlockSpec((d, tn), lambda i, j: (0, j))],
        out_specs=pl.BlockSpec((tm, tn), lambda i, j: (i, j)),
        out_shape=jax.ShapeDtypeStruct((n, c), F32),
        scratch_shapes=[pltpu.VMEM((tm, d), BF16)],
        compiler_params=_params("parallel", "arbitrary"),
        name="norm_matmul",
    )(x, g.reshape(1, d), w)


def _out_kernel(*refs, widths):
    n_in = len(widths)
    mix_refs = refs[:n_in]
    w_ref, x_ref, g_ref, o_ref = refs[n_in:]
    y = None
    off = 0
    for r, wd in zip(mix_refs, widths):
        part = jnp.dot(r[...], w_ref[off:off + wd, :], preferred_element_type=F32)
        y = part if y is None else y + part
        off += wd
    y = y * lax.rsqrt(jnp.mean(y * y, axis=-1, keepdims=True) + EPS) * g_ref[...]
    o_ref[...] = x_ref[...] + y


def out_proj(mix_parts, w, x, g, tm):
    n, d = x.shape
    widths = tuple(p.shape[1] for p in mix_parts)
    in_specs = [pl.BlockSpec((tm, wd), lambda i: (i, 0)) for wd in widths]
    in_specs += [pl.BlockSpec(w.shape, lambda i: (0, 0)),
                 pl.BlockSpec((tm, d), lambda i: (i, 0)),
                 pl.BlockSpec((1, d), lambda i: (0, 0))]
    return pl.pallas_call(
        functools.partial(_out_kernel, widths=widths),
        grid=(n // tm,),
        in_specs=in_specs,
        out_specs=pl.BlockSpec((tm, d), lambda i: (i, 0)),
        out_shape=jax.ShapeDtypeStruct((n, d), F32),
        compiler_params=_params("parallel"),
        name="out_proj",
    )(*mix_parts, w, x, g.reshape(1, d))


def _gdn_prompt_kernel(qkv_ref, z_ref, ba_ref, cw_ref, alog_ref, dtb_ref, onorm_ref,
                       o_ref, sfin_ref, cnew_ref,
                       xp_s, q_s, k_s, v_s, bg_s, u_s, w_s, qg_s, at_s, kdt_s, egl_s, st_s, oc_s,
                       *, tb):
    nc = tb // GDN_CHUNK
    cs = GDN_CHUNK
    t = pl.program_id(1)

    @pl.when(t == 0)
    def _():
        st_s[...] = jnp.zeros_like(st_s)
        xp_s[0:8, :] = jnp.zeros((8, GDN_QKV), F32)

    x = qkv_ref[0]
    xp_s[8:8 + tb, :] = x
    cw = cw_ref[...]
    y = x * cw[3:4, :]
    for j in range(3):
        y = y + xp_s[pl.ds(5 + j, tb), :] * cw[j:j + 1, :]
    xp_s[0:8, :] = xp_s[pl.ds(tb, 8), :]
    cnew_ref[0] = qkv_ref[0, pl.ds(tb - 3, 3), :]
    c = _silu(y)
    for h in range(HEADS):
        qh = c[:, h * HD:(h + 1) * HD]
        kh = c[:, GW + h * HD:GW + (h + 1) * HD]
        q_s[:, h * HD:(h + 1) * HD] = qh * lax.rsqrt(jnp.sum(qh * qh, axis=-1, keepdims=True) + EPS) * SCALE
        k_s[:, h * HD:(h + 1) * HD] = kh * lax.rsqrt(jnp.sum(kh * kh, axis=-1, keepdims=True) + EPS)
    v_s[...] = c[:, 2 * GW:3 * GW]
    ba = ba_ref[0]
    bg_s[0] = jax.nn.sigmoid(ba)
    bg_s[1] = -jnp.exp(alog_ref[...]) * _softplus(ba + dtb_ref[...])

    ri = lax.broadcasted_iota(jnp.int32, (cs, cs), 0)
    ci_ = lax.broadcasted_iota(jnp.int32, (cs, cs), 1)
    incl = ri >= ci_
    strict = ri > ci_
    eye_f = jnp.where(ri == ci_, 1.0, 0.0).astype(F32)

    def prep(ci, carry):
        r0 = pl.multiple_of(ci * cs, cs)
        beta_all = bg_s[0, pl.ds(r0, cs), :]
        g_all = bg_s[1, pl.ds(r0, cs), :]
        for h in range(HEADS):
            hs = slice(h * HD, (h + 1) * HD)
            qh = q_s[pl.ds(r0, cs), hs]
            kh = k_s[pl.ds(r0, cs), hs]
            vh = v_s[pl.ds(r0, cs), hs]
            beta = beta_all[:, h:h + 1]
            g = g_all[:, 4 + h:5 + h]
            gc_row = jnp.sum(jnp.where(ri <= ci_, jnp.broadcast_to(g, (cs, cs)), 0.0), axis=0, keepdims=True)
            gc_col = jnp.sum(jnp.where(ri == ci_, jnp.broadcast_to(gc_row, (cs, cs)), 0.0), axis=1, keepdims=True)
            diff = gc_col - gc_row
            decay = jnp.where(incl, jnp.exp(jnp.where(incl, diff, 0.0)), 0.0)
            kb = kh * beta
            a_low = _dot_nt(kb, kh) * jnp.where(strict, decay, 0.0)
            pw = -a_low
            tinv = eye_f + pw
            for _ in range(5):
                pw = _dot_f32(pw, pw)
                tinv = tinv + _dot_f32(tinv, pw)
            egc = jnp.exp(gc_col)
            rhs = jnp.concatenate([vh * beta, kb * egc], axis=1)
            sol = _dot_f32(tinv, rhs)
            u_s[pl.ds(r0, cs), hs] = sol[:, :HD]
            w_s[pl.ds(r0, cs), hs] = sol[:, HD:]
            at_s[h, pl.ds(r0, cs), :] = _dot_nt(qh, kh) * decay
            qg_s[pl.ds(r0, cs), hs] = qh * egc
            g_last = gc_row[:, cs - 1:cs]
            kdt_s[h, ci] = (kh * jnp.exp(g_last - gc_col)).T
            egl_s[h, pl.ds(ci, 1), :] = jnp.broadcast_to(jnp.exp(g_last), (1, HD))
        return carry

    lax.fori_loop(0, nc, prep, 0)

    def scan(ci, carry):
        r0 = pl.multiple_of(ci * cs, cs)
        for h in range(HEADS):
            hs = slice(h * HD, (h + 1) * HD)
            s_old = st_s[h]
            sb = s_old.astype(BF16)
            v_new = u_s[pl.ds(r0, cs), hs] - _dot(w_s[pl.ds(r0, cs), hs], sb)
            oc_s[pl.ds(r0, cs), hs] = _dot(qg_s[pl.ds(r0, cs), hs], sb) + _dot(at_s[h, pl.ds(r0, cs), :], v_new)
            st_s[h] = s_old * egl_s[h, pl.ds(ci, 1), :] + _dot(kdt_s[h, ci], v_new)
        return carry

    lax.fori_loop(0, nc, scan, 0)

    sfin_ref[0] = st_s[...]
    z = z_ref[0]
    for h in range(HEADS):
        hs = slice(h * HD, (h + 1) * HD)
        oh = oc_s[:, hs]
        on = oh * lax.rsqrt(jnp.mean(oh * oh, axis=-1, keepdims=True) + EPS) * onorm_ref[...]
        o_ref[0, :, hs] = (on * _silu(z[:, hs])).astype(o_ref.dtype)


def gdn_prompt(p3, ba3, conv_w, alog_pad, dtb_pad, o_norm, tb):
    b, t, _ = p3.shape
    nc = tb // GDN_CHUNK
    return pl.pallas_call(
        functools.partial(_gdn_prompt_kernel, tb=tb),
        grid=(b, t // tb),
        in_specs=[pl.BlockSpec((1, tb, GDN_QKV), lambda i, j: (i, j, 0)),
                  pl.BlockSpec((1, tb, GW), lambda i, j: (i, j, 3)),
                  pl.BlockSpec((1, tb, 128), lambda i, j: (i, j, 0)),
                  pl.BlockSpec((4, GDN_QKV), lambda i, j: (0, 0)),
                  pl.BlockSpec((1, 128), lambda i, j: (0, 0)),
                  pl.BlockSpec((1, 128), lambda i, j: (0, 0)),
                  pl.BlockSpec((1, HD), lambda i, j: (0, 0))],
        out_specs=[pl.BlockSpec((1, tb, GW), lambda i, j: (i, j, 0)),
                   pl.BlockSpec((1, HEADS, HD, HD), lambda i, j: (i, 0, 0, 0)),
                   pl.BlockSpec((1, 3, GDN_QKV), lambda i, j: (i, 0, 0))],
        out_shape=[jax.ShapeDtypeStruct((b, t, GW), BF16),
                   jax.ShapeDtypeStruct((b, HEADS, HD, HD), F32),
                   jax.ShapeDtypeStruct((b, 3, GDN_QKV), F32)],
        scratch_shapes=[pltpu.VMEM((tb + 8, GDN_QKV), F32),
                        pltpu.VMEM((tb, GW), F32),
                        pltpu.VMEM((tb, GW), F32),
                        pltpu.VMEM((tb, GW), F32),
                        pltpu.VMEM((2, tb, 128), F32),
                        pltpu.VMEM((tb, GW), F32),
                        pltpu.VMEM((tb, GW), F32),
                        pltpu.VMEM((tb, GW), F32),
                        pltpu.VMEM((HEADS, tb, GDN_CHUNK), F32),
                        pltpu.VMEM((HEADS, nc, HD, GDN_CHUNK), F32),
                        pltpu.VMEM((HEADS, nc, HD), F32),
                        pltpu.VMEM((HEADS, HD, HD), F32),
                        pltpu.VMEM((tb, GW), F32)],
        compiler_params=_params("parallel", "arbitrary"),
        name="gdn_prompt",
    )(p3, p3, ba3, conv_w, alog_pad, dtb_pad, o_norm.reshape(1, HD))


def _gdn_sample_kernel(p_ref, ba_ref, buf_ref, s_ref, cw_ref, alog_ref, dtb_ref, onorm_ref,
                       o_ref, snew_ref, cnew_ref):
    x = p_ref[0, :, 0:GDN_QKV]
    buf = buf_ref[0]
    cw = cw_ref[...]
    y = x * cw[3:4, :]
    for j in range(3):
        y = y + buf[j:j + 1, :] * cw[j:j + 1, :]
    cnew_ref[0, 0:2, :] = buf[1:3, :]
    cnew_ref[0, 2:3, :] = x
    c = _silu(y)
    ba = ba_ref[0]
    beta_all = jax.nn.sigmoid(ba)
    g_all = -jnp.exp(alog_ref[...]) * _softplus(ba + dtb_ref[...])
    z = p_ref[0, :, GDN_QKV:GDN_QKV + GW]
    for h in range(HEADS):
        hs = slice(h * HD, (h + 1) * HD)
        qh = c[:, h * HD:(h + 1) * HD]
        kh = c[:, GW + h * HD:GW + (h + 1) * HD]
        vh = c[:, 2 * GW + h * HD:2 * GW + (h + 1) * HD]
        qh = qh * lax.rsqrt(jnp.sum(qh * qh, axis=-1, keepdims=True) + EPS) * SCALE
        kh = kh * lax.rsqrt(jnp.sum(kh * kh, axis=-1, keepdims=True) + EPS)
        qc = _row_to_col(qh, HD)
        kc = _row_to_col(kh, HD)
        s = s_ref[0, h] * jnp.exp(g_all[:, 4 + h:5 + h])
        u = (vh - jnp.sum(kc * s, axis=0, keepdims=True)) * beta_all[:, h:h + 1]
        s = s + kc * u
        snew_ref[0, h] = s
        oh = jnp.sum(qc * s, axis=0, keepdims=True)
        on = oh * lax.rsqrt(jnp.mean(oh * oh, axis=-1, keepdims=True) + EPS) * onorm_ref[...]
        o_ref[0, :, hs] = (on * _silu(z[:, hs])).astype(o_ref.dtype)


def gdn_sample(p3, ba3, conv_buf, state, conv_w, alog_pad, dtb_pad, o_norm):
    b = p3.shape[0]
    return pl.pallas_call(
        _gdn_sample_kernel,
        grid=(b,),
        in_specs=[pl.BlockSpec((1, 1, EVEN_MAIN), lambda i: (i, 0, 0)),
                  pl.BlockSpec((1, 1, 128), lambda i: (i, 0, 0)),
                  pl.BlockSpec((1, 3, GDN_QKV), lambda i: (i, 0, 0)),
                  pl.BlockSpec((1, HEADS, HD, HD), lambda i: (i, 0, 0, 0)),
                  pl.BlockSpec((4, GDN_QKV), lambda i: (0, 0)),
                  pl.BlockSpec((1, 128), lambda i: (0, 0)),
                  pl.BlockSpec((1, 128), lambda i: (0, 0)),
                  pl.BlockSpec((1, HD), lambda i: (0, 0))],
        out_specs=[pl.BlockSpec((1, 1, GW), lambda i: (i, 0, 0)),
                   pl.BlockSpec((1, HEADS, HD, HD), lambda i: (i, 0, 0, 0)),
                   pl.BlockSpec((1, 3, GDN_QKV), lambda i: (i, 0, 0))],
        out_shape=[jax.ShapeDtypeStruct((b, 1, GW), BF16),
                   jax.ShapeDtypeStruct((b, HEADS, HD, HD), F32),
                   jax.ShapeDtypeStruct((b, 3, GDN_QKV), F32)],
        compiler_params=_params("parallel"),
        name="gdn_sample",
    )(p3, ba3, conv_buf, state, conv_w, alog_pad, dtb_pad, o_norm.reshape(1, HD))


def _swa_group_kernel(*refs, first, last):
    q_ref, kc_ref, kp_ref, vc_ref, vp_ref = refs[:5]
    pos = 5
    if not first:
        acc_ref, m_ref, l_ref = refs[pos:pos + 3]
        pos += 3
    if last:
        gate_ref = refs[pos]
        pos += 1
        o_ref = refs[pos]
    else:
        acc_o, m_o, l_o = refs[pos:pos + 3]
    n = SWA_N
    ri = lax.broadcasted_iota(jnp.int32, (n, n), 0)
    ci_ = lax.broadcasted_iota(jnp.int32, (n, n), 1)
    mask_c = ci_ <= ri
    mask_p = (ci_ - ri) >= jnp.where(pl.program_id(2) > 0, 0, n)
    for h in range(HEADS):
        hs = slice(h * HD, (h + 1) * HD)
        qh = q_ref[0, :, hs]
        s_c = jnp.where(mask_c, _dot_nt(qh, kc_ref[0, :, hs]) * SCALE, -jnp.inf)
        s_p = jnp.where(mask_p, _dot_nt(qh, kp_ref[0, :, hs]) * SCALE, -jnp.inf)
        m = jnp.maximum(jnp.max(s_c, axis=-1, keepdims=True), jnp.max(s_p, axis=-1, keepdims=True))
        if not first:
            m_old = m_ref[0, :, h * HD:h * HD + 1]
            m = jnp.maximum(m, m_old)
        p_c = jnp.exp(s_c - m)
        p_p = jnp.exp(s_p - m)
        l = jnp.sum(p_c, axis=-1, keepdims=True) + jnp.sum(p_p, axis=-1, keepdims=True)
        acc = _dot(p_c, vc_ref[0, :, hs]) + _dot(p_p, vp_ref[0, :, hs])
        if not first:
            corr = jnp.exp(m_old - m)
            l = l + corr * l_ref[0, :, h * HD:h * HD + 1]
            acc = acc + corr * acc_ref[0, :, hs]
        if last:
            o_ref[0, :, hs] = (acc / l * _silu(gate_ref[0, :, hs])).astype(o_ref.dtype)
        else:
            acc_o[0, :, hs] = acc
            m_o[0, :, hs] = jnp.broadcast_to(m, (n, HD))
            l_o[0, :, hs] = jnp.broadcast_to(l, (n, HD))


def swa_group(p3, gi, dil, state, last):
    b, t, cols = p3.shape
    ts = t // dil
    nq = ts // SWA_N
    pv = p3.reshape(b, ts, dil * cols)
    cb = cols // GW

    def col(which):
        return 4 + which * 3 + gi

    def spec_p(which, prev):
        if prev:
            return pl.BlockSpec((1, SWA_N, GW), lambda i, r, j: (i, jnp.maximum(j - 1, 0), r * cb + col(which)))
        return pl.BlockSpec((1, SWA_N, GW), lambda i, r, j: (i, j, r * cb + col(which)))

    spec_s = pl.BlockSpec((1, SWA_N, GW), lambda i, r, j: (i, j, r))
    in_specs = [spec_p(0, False), spec_p(1, False), spec_p(1, True), spec_p(2, False), spec_p(2, True)]
    args = [pv] * 5
    first = state is None
    if not first:
        in_specs += [spec_s] * 3
        args += [a.reshape(b, ts, dil * GW) for a in state]
    if last:
        in_specs.append(pl.BlockSpec((1, SWA_N, GW), lambda i, r, j: (i, j, r * cb + 13)))
        args.append(pv)
        out_specs = spec_s
        out_shape = jax.ShapeDtypeStruct((b, ts, dil * GW), BF16)
    else:
        out_specs = [spec_s] * 3
        out_shape = [jax.ShapeDtypeStruct((b, ts, dil * GW), F32)] * 3
    out = pl.pallas_call(
        functools.partial(_swa_group_kernel, first=first, last=last),
        grid=(b, dil, nq),
        in_specs=in_specs,
        out_specs=out_specs,
        out_shape=out_shape,
        compiler_params=_params("parallel", "parallel", "arbitrary"),
        name=f"swa_group{gi}",
    )(*args)
    if last:
        return out.reshape(b, t, GW)
    return tuple(a.reshape(b, t, GW) for a in out)


def _mem_prompt_kernel(q_ref, gate_ref, kv_ref, o_ref):
    for h in range(HEADS):
        hs = slice(h * HD, (h + 1) * HD)
        s = _dot_nt(q_ref[0, :, hs], kv_ref[0, :, hs]) * SCALE
        m = jnp.max(s, axis=-1, keepdims=True)
        p = jnp.exp(s - m)
        l = jnp.sum(p, axis=-1, keepdims=True)
        o = _dot(p, kv_ref[0, :, GW + h * HD:GW + (h + 1) * HD]) / l
        o_ref[0, :, hs] = (o * _silu(gate_ref[0, :, hs])).astype(o_ref.dtype)


def mem_attend_prompt(p3, q_blk, mkv, tq):
    b, t, _ = p3.shape
    return pl.pallas_call(
        _mem_prompt_kernel,
        grid=(b, t // tq),
        in_specs=[pl.BlockSpec((1, tq, GW), lambda i, j: (i, j, q_blk)),
                  pl.BlockSpec((1, tq, GW), lambda i, j: (i, j, q_blk + 1)),
                  pl.BlockSpec((1, MEM_LEN, 2 * GW), lambda i, j: (i, 0, 0))],
        out_specs=pl.BlockSpec((1, tq, GW), lambda i, j: (i, j, 0)),
        out_shape=jax.ShapeDtypeStruct((b, t, GW), BF16),
        compiler_params=_params("parallel", "parallel"),
        name="mem_prompt",
    )(p3, p3, mkv)


def _attend_rows(qh, kbuf, vbuf, k_new=None, v_new=None):
    s = jnp.sum(kbuf * qh, axis=-1, keepdims=True) * SCALE
    m = jnp.max(s, axis=0, keepdims=True)
    if k_new is not None:
        s_new = jnp.sum(k_new * qh, axis=-1, keepdims=True) * SCALE
        m = jnp.maximum(m, s_new)
    p = jnp.exp(s - m)
    l = jnp.sum(p, axis=0, keepdims=True)
    num = jnp.sum(p * vbuf, axis=0, keepdims=True)
    if k_new is not None:
        p_new = jnp.exp(s_new - m)
        l = l + p_new
        num = num + p_new * v_new
    return num, m, l


def _swa_sample_kernel(p_ref, b1_ref, b2_ref, b3_ref, o_ref):
    bufs = (b1_ref, b2_ref, b3_ref)
    for h in range(HEADS):
        parts = []
        for gi in range(3):
            def colsl(which):
                o = 4 * GW + (which * 3 + gi) * GW + h * HD
                return slice(o, o + HD)
            qh = p_ref[0, :, colsl(0)]
            parts.append(_attend_rows(qh, bufs[gi][0, :, h * HD:(h + 1) * HD],
                                      bufs[gi][0, :, GW + h * HD:GW + (h + 1) * HD],
                                      p_ref[0, :, colsl(1)], p_ref[0, :, colsl(2)]))
        m_all = jnp.maximum(jnp.maximum(parts[0][1], parts[1][1]), parts[2][1])
        num = jnp.zeros((1, HD), F32)
        den = jnp.zeros((1, 1), F32)
        for nu, m, l in parts:
            cfac = jnp.exp(m - m_all)
            num = num + nu * cfac
            den = den + l * cfac
        gate = p_ref[0, :, 13 * GW + h * HD:13 * GW + (h + 1) * HD]
        o_ref[0, :, h * HD:(h + 1) * HD] = (num / den * _silu(gate)).astype(o_ref.dtype)


def swa_sample(p3, bufs):
    b = p3.shape[0]
    in_specs = [pl.BlockSpec((1, 1, EVEN_MAIN), lambda i: (i, 0, 0))]
    args = [p3]
    for (win, dil), buf in zip(SWA_GROUPS, bufs):
        length = buf.shape[1]
        in_specs.append(pl.BlockSpec((1, length // dil, 2 * GW), lambda i: (i, 0, 0)))
        args.append(buf.reshape(b, length // dil, dil * 2 * GW))
    return pl.pallas_call(
        _swa_sample_kernel,
        grid=(b,),
        in_specs=in_specs,
        out_specs=pl.BlockSpec((1, 1, GW), lambda i: (i, 0, 0)),
        out_shape=jax.ShapeDtypeStruct((b, 1, GW), BF16),
        compiler_params=_params("parallel"),
        name="swa_sample",
    )(*args)


def _mem_sample_kernel(q_ref, gate_ref, kv_ref, o_ref):
    for h in range(HEADS):
        hs = slice(h * HD, (h + 1) * HD)
        num, _, l = _attend_rows(q_ref[0, :, hs], kv_ref[0, :, hs], kv_ref[0, :, GW + h * HD:GW + (h + 1) * HD])
        o_ref[0, :, hs] = (num / l * _silu(gate_ref[0, :, hs])).astype(o_ref.dtype)


def mem_sample(p3, q_blk, kv):
    b = p3.shape[0]
    return pl.pallas_call(
        _mem_sample_kernel,
        grid=(b,),
        in_specs=[pl.BlockSpec((1, 1, GW), lambda i: (i, 0, q_blk)),
                  pl.BlockSpec((1, 1, GW), lambda i: (i, 0, q_blk + 1)),
                  pl.BlockSpec((1, MEM_LEN, 2 * GW), lambda i: (i, 0, 0))],
        out_specs=pl.BlockSpec((1, 1, GW), lambda i: (i, 0, 0)),
        out_shape=jax.ShapeDtypeStruct((b, 1, GW), BF16),
        compiler_params=_params("parallel"),
        name="mem_sample",
    )(p3, p3, kv)


def _lru_gates(xc, wa_ref, ba_ref, wx_ref, bx_ref, lam_ref):
    rs, is_ = [], []
    for n in range(LRU_BLOCKS):
        xb = xc[:, n * LRU_BW:(n + 1) * LRU_BW].astype(BF16)
        rs.append(jnp.dot(xb, wa_ref[n], preferred_element_type=F32))
        is_.append(jnp.dot(xb, wx_ref[n], preferred_element_type=F32))
    r = jax.nn.sigmoid(jnp.concatenate(rs, axis=1) + ba_ref[...])
    i = jax.nn.sigmoid(jnp.concatenate(is_, axis=1) + bx_ref[...])
    log_a = -LRU_C * r * _softplus(-lam_ref[...])
    a = jnp.exp(log_a)
    th = jnp.tanh(log_a)
    b = jnp.sqrt(-2.0 * th / (1.0 - th)) * (i * xc)
    return a, b


def _lru_prompt_kernel(x_ref, gate_ref, cw_ref, cb_ref, wa_ref, ba_ref, wx_ref, bx_ref, lam_ref,
                       o_ref, hlast_ref, cnew_ref, xp_s, a_s, b_s, carry_s, *, tb):
    t = pl.program_id(1)

    @pl.when(t == 0)
    def _():
        carry_s[...] = jnp.zeros_like(carry_s)
        xp_s[0:8, :] = jnp.zeros((8, LRU_WIDTH), F32)

    x = x_ref[0]
    xp_s[8:8 + tb, :] = x
    cw = cw_ref[...]
    y = x * cw[3:4, :]
    for j in range(3):
        y = y + xp_s[pl.ds(5 + j, tb), :] * cw[j:j + 1, :]
    xp_s[0:8, :] = xp_s[pl.ds(tb, 8), :]
    cnew_ref[0] = x_ref[0, pl.ds(tb - 3, 3), :]
    xc = y + cb_ref[...]
    a, b = _lru_gates(xc, wa_ref, ba_ref, wx_ref, bx_ref, lam_ref)
    r8 = lax.broadcasted_iota(jnp.int32, (tb, LRU_WIDTH), 0) & 7
    for s in (1, 2, 4):
        keep = r8 >= s
        b = jnp.where(keep, b + a * pltpu.roll(b, s, axis=0), b)
        a = jnp.where(keep, a * pltpu.roll(a, s, axis=0), a)
    a_s[...] = a
    b_s[...] = b

    def group(gi, h_prev):
        r0 = pl.multiple_of(gi * 8, 8)
        hg = a_s[pl.ds(r0, 8), :] * h_prev + b_s[pl.ds(r0, 8), :]
        b_s[pl.ds(r0, 8), :] = hg
        return jnp.broadcast_to(hg[7:8, :], (8, LRU_WIDTH))

    h_end = lax.fori_loop(0, tb // 8, group, carry_s[...])
    carry_s[...] = h_end
    hlast_ref[0] = h_end[0:1, :]
    o_ref[0] = (b_s[...] * _silu(gate_ref[0])).astype(o_ref.dtype)


def lru_prompt(p3, conv_w, conv_b, wa, ba, wx, bx, lam, tb):
    b, t, _ = p3.shape
    vec = pl.BlockSpec((1, LRU_WIDTH), lambda i, j: (0, 0))
    wspec = pl.BlockSpec((LRU_BLOCKS, LRU_BW, LRU_BW), lambda i, j: (0, 0, 0))
    return pl.pallas_call(
        functools.partial(_lru_prompt_kernel, tb=tb),
        grid=(b, t // tb),
        in_specs=[pl.BlockSpec((1, tb, LRU_WIDTH), lambda i, j: (i, j, 0)),
                  pl.BlockSpec((1, tb, LRU_WIDTH), lambda i, j: (i, j, 1)),
                  pl.BlockSpec((4, LRU_WIDTH), lambda i, j: (0, 0)),
                  vec, wspec, vec, wspec, vec, vec],
        out_specs=[pl.BlockSpec((1, tb, LRU_WIDTH), lambda i, j: (i, j, 0)),
                   pl.BlockSpec((1, 1, LRU_WIDTH), lambda i, j: (i, 0, 0)),
                   pl.BlockSpec((1, 3, LRU_WIDTH), lambda i, j: (i, 0, 0))],
        out_shape=[jax.ShapeDtypeStruct((b, t, LRU_WIDTH), BF16),
                   jax.ShapeDtypeStruct((b, 1, LRU_WIDTH), F32),
                   jax.ShapeDtypeStruct((b, 3, LRU_WIDTH), F32)],
        scratch_shapes=[pltpu.VMEM((tb + 8, LRU_WIDTH), F32),
                        pltpu.VMEM((tb, LRU_WIDTH), F32),
                        pltpu.VMEM((tb, LRU_WIDTH), F32),
                        pltpu.VMEM((8, LRU_WIDTH), F32)],
        compiler_params=_params("parallel", "arbitrary"),
        name="lru_prompt",
    )(p3, p3, conv_w, conv_b.reshape(1, -1), wa, ba.reshape(1, -1), wx, bx.reshape(1, -1), lam.reshape(1, -1))


def _lru_sample_kernel(p_ref, buf_ref, h0_ref, cw_ref, cb_ref, wa_ref, ba_ref, wx_ref, bx_ref, lam_ref,
                       o_ref, hnew_ref, cnew_ref):
    x = p_ref[:, 0:LRU_WIDTH]
    cw = cw_ref[...]
    y = x * cw[3:4, :]
    for j in range(3):
        y = y + buf_ref[j] * cw[j:j + 1, :]
    cnew_ref[0] = buf_ref[1]
    cnew_ref[1] = buf_ref[2]
    cnew_ref[2] = x
    xc = y + cb_ref[...]
    a, b = _lru_gates(xc, wa_ref, ba_ref, wx_ref, bx_ref, lam_ref)
    h = a * h0_ref[...] + b
    hnew_ref[...] = h
    o_ref[...] = (h * _silu(p_ref[:, LRU_WIDTH:2 * LRU_WIDTH])).astype(o_ref.dtype)


def lru_sample(p2, conv_buf_t, h0, conv_w, conv_b, wa, ba, wx, bx, lam):
    b = p2.shape[0]
    return pl.pallas_call(
        _lru_sample_kernel,
        out_shape=[jax.ShapeDtypeStruct((b, LRU_WIDTH), BF16),
                   jax.ShapeDtypeStruct((b, LRU_WIDTH), F32),
                   jax.ShapeDtypeStruct((3, b, LRU_WIDTH), F32)],
        compiler_params=pltpu.CompilerParams(vmem_limit_bytes=VMEM_LIMIT),
        name="lru_sample",
    )(p2, conv_buf_t, h0, conv_w, conv_b.reshape(1, -1), wa, ba.reshape(1, -1), wx, bx.reshape(1, -1),
      lam.reshape(1, -1))


def _even_layer(xp, xs, mkv, w_in, w_out, conv_w, a_log, dt_bias, o_norm, g_pre, g_post,
                state, conv_buf, swa_bufs, mem_kv_s, bp, tp):
    n_p = bp * tp
    bs = xs.shape[0]
    w_main = jnp.concatenate([w_in[:, :2048], w_in[:, 2056:]], axis=1).astype(BF16)
    w_ba = jnp.pad(w_in[:, 2048:2056], ((0, 0), (0, 120))).astype(BF16)
    alog_pad = jnp.pad(a_log, (4, 120)).reshape(1, 128)
    dtb_pad = jnp.pad(dt_bias, (4, 120)).reshape(1, 128)
    w_out_b = w_out.astype(BF16)

    pp = norm_matmul(xp, g_pre, w_main, 1024, 1024).reshape(bp, tp, EVEN_MAIN)
    bap = norm_matmul(xp, g_pre, w_ba, 1024, 128).reshape(bp, tp, 128)
    o_a, s_p, c_p = gdn_prompt(pp, bap, conv_w, alog_pad, dtb_pad, o_norm, 512)
    st = None
    for gi, (win, dil) in enumerate(SWA_GROUPS):
        st = swa_group(pp, gi, dil, st, gi == 2)
    o_b = st
    o_m = mem_attend_prompt(pp, 14, mkv, 512)
    xp_new = out_proj([o_a.reshape(n_p, GW), o_b.reshape(n_p, GW), o_m.reshape(n_p, GW)], w_out_b, xp, g_post, 512)
    swa_p = []
    for gi, (win, dil) in enumerate(SWA_GROUPS):
        keep = min(win, tp)
        kcol = (4 + 3 + gi) * GW
        vcol = (4 + 6 + gi) * GW
        kk = pp[:, tp - keep:, kcol:kcol + GW].reshape(bp, keep, 1, HEADS, HD)
        vv = pp[:, tp - keep:, vcol:vcol + GW].reshape(bp, keep, 1, HEADS, HD)
        swa_p.append(jnp.concatenate([kk, vv], axis=2))

    ps = norm_matmul(xs, g_pre, w_main, bs, 1024).reshape(bs, 1, EVEN_MAIN)
    bas = norm_matmul(xs, g_pre, w_ba, bs, 128).reshape(bs, 1, 128)
    o_as, s_s, c_s = gdn_sample(ps, bas, conv_buf, state, conv_w, alog_pad, dtb_pad, o_norm)
    o_bs = swa_sample(ps, swa_bufs)
    o_ms = mem_sample(ps, 14, mem_kv_s.reshape(bs, MEM_LEN, 2 * GW))
    xs_new = out_proj([o_as.reshape(bs, GW), o_bs.reshape(bs, GW), o_ms.reshape(bs, GW)], w_out_b, xs, g_post, bs)
    swa_s = []
    for gi, buf in enumerate(swa_bufs):
        kcol = (4 + 3 + gi) * GW
        vcol = (4 + 6 + gi) * GW
        new = jnp.stack([ps[:, :, kcol:kcol + GW].reshape(bs, 1, HEADS, HD),
                         ps[:, :, vcol:vcol + GW].reshape(bs, 1, HEADS, HD)], axis=2)
        swa_s.append(jnp.concatenate([buf[:, 1:], new], axis=1))
    return xp_new, xs_new, s_p, s_s, c_p, c_s, swa_p, swa_s


def _odd_layer(xp, xs, mkv, w_in, w_out, conv_w, conv_b, wa, ba, wx, bx, lam, g_pre, g_post,
               h0, conv_buf, mem_kv_s, bp, tp):
    n_p = bp * tp
    bs = xs.shape[0]
    w_in_b = w_in.astype(BF16)
    w_out_b = w_out.astype(BF16)
    wa_b = wa.astype(BF16)
    wx_b = wx.astype(BF16)

    pp = norm_matmul(xp, g_pre, w_in_b, 1024, 1024).reshape(bp, tp, ODD_IN)
    o_c, h_p, c_p = lru_prompt(pp, conv_w, conv_b, wa_b, ba, wx_b, bx, lam, 512)
    o_m = mem_attend_prompt(pp, 4, mkv, 512)
    xp_new = out_proj([o_c.reshape(n_p, LRU_WIDTH), o_m.reshape(n_p, GW)], w_out_b, xp, g_post, 512)

    ps = norm_matmul(xs, g_pre, w_in_b, bs, 1024)
    o_cs, h_s, c_s_t = lru_sample(ps, jnp.swapaxes(conv_buf, 0, 1), h0, conv_w, conv_b, wa_b, ba, wx_b, bx, lam)
    o_ms = mem_sample(ps.reshape(bs, 1, ODD_IN), 4, mem_kv_s.reshape(bs, MEM_LEN, 2 * GW))
    xs_new = out_proj([o_cs, o_ms.reshape(bs, GW)], w_out_b, xs, g_post, bs)
    return xp_new, xs_new, h_p.reshape(bp, LRU_WIDTH), h_s, c_p, jnp.swapaxes(c_s_t, 0, 1)


def kernel(x_prompt, x_sample, state_gdn, state_gdn_conv, cache_swa1, cache_swa2, cache_swa3, state_lru, state_lru_conv, cache_mem, mem_prompt, norm_pre, norm_post, mem_norm, w_mem_kv, w_in_even, w_out_even, gdn_conv_w, gdn_a_log, gdn_dt_bias, gdn_norm, w_in_odd, w_out_odd, lru_conv_w, lru_conv_b, lru_wa, lru_ba, lru_wx, lru_bx, lru_lambda):
    bp, tp, d = x_prompt.shape
    bs = x_sample.shape[0]
    depth = norm_pre.shape[0]
    xp = x_prompt.reshape(bp * tp, d)
    xs = x_sample.reshape(bs, d)
    mem_rows = mem_prompt.reshape(bp * MEM_LEN, d)
    swa_in = (cache_swa1, cache_swa2, cache_swa3)
    gdn_p, gdn_s, gconv_p, gconv_s = [], [], [], []
    swa_p = ([], [], [])
    swa_s = ([], [], [])
    lru_p, lru_s, lconv_p, lconv_s, mem_p = [], [], [], [], []
    for layer in range(depth):
        j = layer // 2
        mkv = norm_matmul(mem_rows, mem_norm[layer], w_mem_kv[layer].astype(BF16), 512, 1024)
        mkv = mkv.reshape(bp, MEM_LEN, 2 * GW)
        mem_p.append(mkv.reshape(bp, MEM_LEN, 2, HEADS, HD))
        if layer % 2 == 0:
            xp, xs, s_p, s_s, c_p, c_s, b_p, b_s = _even_layer(
                xp, xs, mkv, w_in_even[j], w_out_even[j], gdn_conv_w[j], gdn_a_log[j], gdn_dt_bias[j],
                gdn_norm[j], norm_pre[layer], norm_post[layer], state_gdn[j], state_gdn_conv[j],
                [c[j] for c in swa_in], cache_mem[layer], bp, tp)
            gdn_p.append(s_p)
            gdn_s.append(s_s)
            gconv_p.append(c_p)
            gconv_s.append(c_s)
            for gi in range(3):
                swa_p[gi].append(b_p[gi])
                swa_s[gi].append(b_s[gi])
        else:
            xp, xs, h_p, h_s, c_p, c_s = _odd_layer(
                xp, xs, mkv, w_in_odd[j], w_out_odd[j], lru_conv_w[j], lru_conv_b[j], lru_wa[j], lru_ba[j],
                lru_wx[j], lru_bx[j], lru_lambda[j], norm_pre[layer], norm_post[layer],
                state_lru[j], state_lru_conv[j], cache_mem[layer], bp, tp)
            lru_p.append(h_p)
            lru_s.append(h_s)
            lconv_p.append(c_p)
            lconv_s.append(c_s)
    return (xp.reshape(bp, tp, d), xs.reshape(bs, 1, d),
            jnp.stack(gdn_p), jnp.stack(gdn_s), jnp.stack(gconv_p), jnp.stack(gconv_s),
            jnp.stack(swa_p[0]), jnp.stack(swa_s[0]), jnp.stack(swa_p[1]), jnp.stack(swa_s[1]),
            jnp.stack(swa_p[2]), jnp.stack(swa_s[2]),
            jnp.stack(lru_p), jnp.stack(lru_s), jnp.stack(lconv_p), jnp.stack(lconv_s),
            jnp.stack(mem_p))
```
